```python
import jax, jax.numpy as jnp
from jax import lax
import numpy as np

D_MODEL = 1024
BATCH = 4
SEQ = 4096
DEPTH = 4
DEC_BATCH = 128
DEC_SEQ = 4
PAST_LEN = 8192
PAGE_SIZE = 128

N_HEADS = 16
N_KV_HEADS = 4
HEAD_DIM = 64
GROUP = N_HEADS // N_KV_HEADS
WINDOW = 128
N_IDX_HEADS = 8
IDX_DIM = 64
TOPK_MAX = 256
Q_BLOCK = 128
D_FF = 4 * D_MODEL
ROPE_THETA = 10000.0
EPS = 1e-6
N_MIXERS = 2
N_SWA_LAYERS = (DEPTH + 1) // 2
N_DSA_LAYERS = DEPTH // 2
Q_DIM = N_HEADS * HEAD_DIM
KV_DIM = N_KV_HEADS * HEAD_DIM
QKV_DIM = Q_DIM + 2 * KV_DIM
DSA_IN_DIM = QKV_DIM + N_IDX_HEADS * IDX_DIM + IDX_DIM + N_IDX_HEADS

kernel_name = 'hybrid_swa_sink_dsa_adaln_step'


def _rmsnorm(x, g):
    xf = x.astype(jnp.float32)
    y = xf * lax.rsqrt(jnp.mean(xf * xf, axis=-1, keepdims=True) + EPS)
    return (y * g.astype(jnp.float32)).astype(x.dtype)


def _rope(x, pos):
    d2 = x.shape[-1] // 2
    inv = ROPE_THETA ** (-jnp.arange(d2, dtype=jnp.float32) / d2)
    ang = pos.astype(jnp.float32)[:, None] * inv[None, :]
    shp = (pos.shape[0],) + (1,) * (x.ndim - 3) + (d2,)
    cos = jnp.cos(ang).reshape(shp)
    sin = jnp.sin(ang).reshape(shp)
    xf = x.astype(jnp.float32)
    x1, x2 = xf[..., :d2], xf[..., d2:]
    return jnp.concatenate([x1 * cos - x2 * sin, x2 * cos + x1 * sin], axis=-1).astype(x.dtype)


def _adaln(c, w, b):
    mod = jax.nn.silu(c) @ w + b
    return jnp.split(mod[:, None, :], 6, axis=-1)


def _modulate(x, g, shift, scale):
    return _rmsnorm(x, g) * (1 + scale) + shift


def _qkv(h, w_in, q_norm, k_norm, pos):
    B, T, _ = h.shape
    proj = h @ w_in
    q = proj[..., :Q_DIM].reshape(B, T, N_KV_HEADS, GROUP, HEAD_DIM)
    k = proj[..., Q_DIM:Q_DIM + KV_DIM].reshape(B, T, N_KV_HEADS, HEAD_DIM)
    v = proj[..., Q_DIM + KV_DIM:QKV_DIM].reshape(B, T, N_KV_HEADS, HEAD_DIM)
    q = _rope(_rmsnorm(q, q_norm), pos)
    k = _rope(_rmsnorm(k, k_norm), pos)
    return q, k, v, proj[..., QKV_DIM:]


def _sink_softmax(s, sink, mask):
    s = jnp.where(mask, s, -jnp.inf)
    sk = jnp.broadcast_to(sink.astype(jnp.float32)[:, :, None, None], s.shape[:-1] + (1,))
    p = jax.nn.softmax(jnp.concatenate([s, sk], axis=-1), axis=-1)
    return p[..., :-1]


def _swa_prompt(q, k, v, sink):
    B, S = q.shape[:2]
    nb = S // WINDOW
    qb = q.reshape(B, nb, WINDOW, N_KV_HEADS, GROUP, HEAD_DIM)
    kb = k.reshape(B, nb, WINDOW, N_KV_HEADS, HEAD_DIM)
    vb = v.reshape(B, nb, WINDOW, N_KV_HEADS, HEAD_DIM)
    pad = ((0, 0), (1, 0), (0, 0), (0, 0), (0, 0))
    kk = jnp.concatenate([jnp.pad(kb[:, :-1], pad), kb], axis=2)
    vv = jnp.concatenate([jnp.pad(vb[:, :-1], pad), vb], axis=2)
    s = jnp.einsum('bnqkgd,bnskd->bnkgqs', qb, kk).astype(jnp.float32) * HEAD_DIM ** -0.5
    i = jnp.arange(WINDOW)[:, None]
    j = jnp.arange(2 * WINDOW)[None, :]
    diff = i + WINDOW - j
    band = (diff >= 0) & (diff < WINDOW)
    mask = band[None] & ((jnp.arange(nb)[:, None, None] > 0) | (j >= WINDOW)[None])
    p = _sink_softmax(s, sink, mask[None, :, None, None])
    o = jnp.einsum('bnkgqs,bnskd->bnqkgd', p.astype(vv.dtype), vv)
    return o.reshape(B, S, Q_DIM)


def _swa_sample(q, k, v, buf_k, buf_v, sink):
    B, T = q.shape[:2]
    wb = buf_k.shape[1]
    kk = jnp.concatenate([buf_k, k], axis=1)
    vv = jnp.concatenate([buf_v, v], axis=1)
    pos_q = PAST_LEN + jnp.arange(T)
    pos_k = jnp.concatenate([PAST_LEN - wb + jnp.arange(wb), pos_q])
    d = pos_q[:, None] - pos_k[None, :]
    mask = (d >= 0) & (d < WINDOW)
    s = jnp.einsum('btkgd,bskd->bkgts', q, kk).astype(jnp.float32) * HEAD_DIM ** -0.5
    p = _sink_softmax(s, sink, mask)
    o = jnp.einsum('bkgts,bskd->btkgd', p.astype(vv.dtype), vv)
    return o.reshape(B, T, Q_DIM), kk[:, -wb:], vv[:, -wb:]


def _indexer_parts(rest, pos):
    B, T = rest.shape[:2]
    q_idx = rest[..., :N_IDX_HEADS * IDX_DIM].reshape(B, T, N_IDX_HEADS, IDX_DIM)
    k_idx = rest[..., N_IDX_HEADS * IDX_DIM:N_IDX_HEADS * IDX_DIM + IDX_DIM]
    w_idx = rest[..., N_IDX_HEADS * IDX_DIM + IDX_DIM:] * N_IDX_HEADS ** -0.5
    return _rope(q_idx, pos), _rope(k_idx, pos), w_idx


def _index_scores(q_idx, w_idx, k_idx, mask):
    r = jax.nn.relu(jnp.einsum('bqhd,bld->bqlh', q_idx, k_idx).astype(jnp.float32) * IDX_DIM ** -0.5)
    sc = jnp.einsum('bqlh,bqh->bql', r, w_idx.astype(jnp.float32))
    return jnp.where(mask, sc, -jnp.inf)


def _gathered_attention(q, k_sel, v_sel, valid):
    s = jnp.einsum('bqngd,bqjnd->bqngj', q, k_sel).astype(jnp.float32) * HEAD_DIM ** -0.5
    s = jnp.where(valid[:, :, None, None, :], s, -jnp.inf)
    p = jax.nn.softmax(s, axis=-1)
    return jnp.einsum('bqngj,bqjnd->bqngd', p.astype(v_sel.dtype), v_sel)


def _take_rows(a, i):
    return jax.vmap(lambda aa, ii: aa[ii])(a, i)


def _dsa_prompt(q, k, v, q_idx, k_idx, w_idx):
    B, S = q.shape[:2]
    nb = S // Q_BLOCK
    topk = min(TOPK_MAX, S // 4)
    pos_k = jnp.arange(S)

    def blocks(a):
        return jnp.moveaxis(a.reshape((B, nb, Q_BLOCK) + a.shape[2:]), 1, 0)

    def one_block(args):
        qb, qib, wb, start = args
        pos_q = start + jnp.arange(Q_BLOCK)
        sc = _index_scores(qib, wb, k_idx, pos_k[None, :] <= pos_q[:, None])
        _, idx = lax.top_k(sc, topk)
        valid = idx <= pos_q[None, :, None]
        return _gathered_attention(qb, _take_rows(k, idx), _take_rows(v, idx), valid)

    starts = jnp.arange(nb) * Q_BLOCK
    o = lax.map(one_block, (blocks(q), blocks(q_idx), blocks(w_idx), starts))
    return jnp.moveaxis(o, 0, 1).reshape(B, S, Q_DIM)


def _dsa_sample(q, k, v, q_idx, k_idx, w_idx, pool_k, pool_v, pool_kidx, page_table):
    B, T = q.shape[:2]
    L = PAST_LEN + T
    topk = min(TOPK_MAX, L // 4)
    kidx_past = pool_kidx[page_table].reshape(B, PAST_LEN, IDX_DIM)
    kidx_all = jnp.concatenate([kidx_past, k_idx], axis=1)
    pos_q = PAST_LEN + jnp.arange(T)
    pos_k = jnp.arange(L)
    sc = _index_scores(q_idx, w_idx, kidx_all, pos_k[None, :] <= pos_q[:, None])
    _, idx = lax.top_k(sc, topk)
    is_new = (idx >= PAST_LEN)[..., None, None]
    ip = jnp.minimum(idx, PAST_LEN - 1)
    phys = _take_rows(page_table, ip // PAGE_SIZE)
    off = ip % PAGE_SIZE
    inew = jnp.clip(idx - PAST_LEN, 0, T - 1)
    k_sel = jnp.where(is_new, _take_rows(k, inew), pool_k[phys, off])
    v_sel = jnp.where(is_new, _take_rows(v, inew), pool_v[phys, off])
    o = _gathered_attention(q, k_sel, v_sel, idx <= pos_q[None, :, None])
    return o.reshape(B, T, Q_DIM)


def _sq_relu_mlp(h, w1, w2):
    return jnp.square(jax.nn.relu(h @ w1)) @ w2


def setup_inputs(seed: int = 0) -> dict:
    key = jax.random.key(seed)
    ks = jax.random.split(key, 32)
    f32 = jnp.float32

    def nrm(k, shp, s=1.0):
        return s * jax.random.normal(k, shp, f32)

    n_pages = PAST_LEN // PAGE_SIZE
    n_used = DEC_BATCH * n_pages
    n_pool = n_used + n_used // 4
    wbuf = min(WINDOW, PAST_LEN)
    page_table = jax.random.permutation(ks[7], n_pool)[:n_used].reshape(DEC_BATCH, n_pages).astype(jnp.int32)
    return {
        'x_prompt': nrm(ks[0], (BATCH, SEQ, D_MODEL)),
        'x_sample': nrm(ks[1], (DEC_BATCH, DEC_SEQ, D_MODEL)),
        'cache_swa_k': nrm(ks[2], (N_SWA_LAYERS, DEC_BATCH, wbuf, N_KV_HEADS, HEAD_DIM)),
        'cache_swa_v': nrm(ks[3], (N_SWA_LAYERS, DEC_BATCH, wbuf, N_KV_HEADS, HEAD_DIM)),
        'cache_dsa_k': nrm(ks[4], (N_DSA_LAYERS, n_pool, PAGE_SIZE, N_KV_HEADS, HEAD_DIM)),
        'cache_dsa_v': nrm(ks[5], (N_DSA_LAYERS, n_pool, PAGE_SIZE, N_KV_HEADS, HEAD_DIM)),
        'cache_dsa_kidx': nrm(ks[6], (N_DSA_LAYERS, n_pool, PAGE_SIZE, IDX_DIM)),
        'page_table': page_table,
        'c_prompt': nrm(ks[8], (BATCH, D_MODEL)),
        'c_sample': nrm(ks[9], (DEC_BATCH, D_MODEL)),
        'ada_w': nrm(ks[10], (DEPTH, D_MODEL, 6 * D_MODEL), 0.5 * D_MODEL ** -0.5),
        'ada_b': nrm(ks[11], (DEPTH, 6 * D_MODEL), 0.02),
        'norm1_g': 1.0 + nrm(ks[12], (DEPTH, D_MODEL), 0.05),
        'norm2_g': 1.0 + nrm(ks[13], (DEPTH, D_MODEL), 0.05),
        'swa_w_in': nrm(ks[14], (N_SWA_LAYERS, D_MODEL, QKV_DIM), D_MODEL ** -0.5),
        'swa_q_norm': 1.0 + nrm(ks[15], (N_SWA_LAYERS, HEAD_DIM), 0.05),
        'swa_k_norm': 1.0 + nrm(ks[16], (N_SWA_LAYERS, HEAD_DIM), 0.05),
        'swa_sinks': nrm(ks[17], (N_SWA_LAYERS, N_HEADS)),
        'swa_w_out': nrm(ks[18], (N_SWA_LAYERS, Q_DIM, D_MODEL), Q_DIM ** -0.5),
        'dsa_w_in': nrm(ks[19], (N_DSA_LAYERS, D_MODEL, DSA_IN_DIM), D_MODEL ** -0.5),
        'dsa_q_norm': 1.0 + nrm(ks[20], (N_DSA_LAYERS, HEAD_DIM), 0.05),
        'dsa_k_norm': 1.0 + nrm(ks[21], (N_DSA_LAYERS, HEAD_DIM), 0.05),
        'dsa_w_out': nrm(ks[22], (N_DSA_LAYERS, Q_DIM, D_MODEL), Q_DIM ** -0.5),
        'mlp_w1': nrm(ks[23], (DEPTH, D_MODEL, D_FF), D_MODEL ** -0.5),
        'mlp_w2': nrm(ks[24], (DEPTH, D_FF, D_MODEL), D_FF ** -0.5),
    }


def reference(x_prompt, x_sample, cache_swa_k, cache_swa_v, cache_dsa_k, cache_dsa_v, cache_dsa_kidx,
              page_table, c_prompt, c_sample, ada_w, ada_b, norm1_g, norm2_g, swa_w_in, swa_q_norm,
              swa_k_norm, swa_sinks, swa_w_out, dsa_w_in, dsa_q_norm, dsa_k_norm, dsa_w_out, mlp_w1, mlp_w2):
    pos_p = jnp.arange(x_prompt.shape[1])
    pos_s = PAST_LEN + jnp.arange(x_sample.shape[1])
    xp, xs = x_prompt, x_sample
    swa_kp, swa_vp, swa_ks, swa_vs = [], [], [], []
    dsa_kp, dsa_vp, dsa_ip, dsa_ks, dsa_vs, dsa_is = [], [], [], [], [], []
    for i in range(DEPTH):
        j = i // N_MIXERS
        sh1p, sc1p, g1p, sh2p, sc2p, g2p = _adaln(c_prompt, ada_w[i], ada_b[i])
        sh1s, sc1s, g1s, sh2s, sc2s, g2s = _adaln(c_sample, ada_w[i], ada_b[i])
        hp = _modulate(xp, norm1_g[i], sh1p, sc1p)
        hs = _modulate(xs, norm1_g[i], sh1s, sc1s)
        if i % N_MIXERS == 0:
            sink = swa_sinks[j].reshape(N_KV_HEADS, GROUP)
            qp, kp, vp, _ = _qkv(hp, swa_w_in[j], swa_q_norm[j], swa_k_norm[j], pos_p)
            qs, ks_, vs_, _ = _qkv(hs, swa_w_in[j], swa_q_norm[j], swa_k_norm[j], pos_s)
            op = _swa_prompt(qp, kp, vp, sink)
            os_, kbuf, vbuf = _swa_sample(qs, ks_, vs_, cache_swa_k[j], cache_swa_v[j], sink)
            wb = cache_swa_k.shape[2]
            swa_kp.append(kp[:, -wb:])
            swa_vp.append(vp[:, -wb:])
            swa_ks.append(kbuf)
            swa_vs.append(vbuf)
            w_out = swa_w_out[j]
        else:
            qp, kp, vp, rp = _qkv(hp, dsa_w_in[j], dsa_q_norm[j], dsa_k_norm[j], pos_p)
            qs, ks_, vs_, rs = _qkv(hs, dsa_w_in[j], dsa_q_norm[j], dsa_k_norm[j], pos_s)
            qip, kip, wip = _indexer_parts(rp, pos_p)
            qis, kis, wis = _indexer_parts(rs, pos_s)
            op = _dsa_prompt(qp, kp, vp, qip, kip, wip)
            os_ = _dsa_sample(qs, ks_, vs_, qis, kis, wis, cache_dsa_k[j], cache_dsa_v[j],
                              cache_dsa_kidx[j], page_table)
            dsa_kp.append(kp)
            dsa_vp.append(vp)
            dsa_ip.append(kip)
            dsa_ks.append(ks_)
            dsa_vs.append(vs_)
            dsa_is.append(kis)
            w_out = dsa_w_out[j]
        xp = xp + g1p * (op @ w_out)
        xs = xs + g1s * (os_ @ w_out)
        hp = _modulate(xp, norm2_g[i], sh2p, sc2p)
        hs = _modulate(xs, norm2_g[i], sh2s, sc2s)
        xp = xp + g2p * _sq_relu_mlp(hp, mlp_w1[i], mlp_w2[i])
        xs = xs + g2s * _sq_relu_mlp(hs, mlp_w1[i], mlp_w2[i])
    return (xp, xs, jnp.stack(swa_kp), jnp.stack(swa_vp), jnp.stack(swa_ks), jnp.stack(swa_vs),
            jnp.stack(dsa_kp), jnp.stack(dsa_vp), jnp.stack(dsa_ip), jnp.stack(dsa_ks), jnp.stack(dsa_vs),
            jnp.stack(dsa_is))
```

```python
import functools

import jax
import jax.numpy as jnp
from jax import lax
from jax.experimental import pallas as pl
from jax.experimental.pallas import tpu as pltpu

D_MODEL = 1024
BATCH = 4
SEQ = 4096
DEPTH = 4
DEC_BATCH = 128
DEC_SEQ = 4
PAST_LEN = 8192
PAGE_SIZE = 128
N_PAGES = PAST_LEN // PAGE_SIZE
N_HEADS = 16
N_KV_HEADS = 4
HEAD_DIM = 64
GROUP = N_HEADS // N_KV_HEADS
WINDOW = 128
N_IDX_HEADS = 8
IDX_DIM = 64
TOPK = 256
D_FF = 4 * D_MODEL
ROPE_THETA = 10000.0
EPS = 1e-6
Q_DIM = N_HEADS * HEAD_DIM
KV_DIM = N_KV_HEADS * HEAD_DIM
QKV_DIM = Q_DIM + 2 * KV_DIM
IDXQ_DIM = N_IDX_HEADS * IDX_DIM
DSA_IN_DIM = QKV_DIM + IDXQ_DIM + IDX_DIM + N_IDX_HEADS

LANES = 128
TP = BATCH * SEQ
TS = DEC_BATCH * DEC_SEQ
TT = TP + TS
TM = 512
TILES_PER_BATCH = SEQ // TM
N_TILES = TT // TM
DSA_ROWS = 2176
KCH = 256
SKEYS = PAST_LEN + LANES
NEG = -1e30
INT_MIN = -(2 ** 31)
VMEM_LIMIT = 56 * 1024 * 1024

F32 = jnp.float32
BF16 = jnp.bfloat16
I32 = jnp.int32


def _cparams(n_axes):
    return pltpu.CompilerParams(dimension_semantics=("arbitrary",) * n_axes, vmem_limit_bytes=VMEM_LIMIT)


def _sortable(x):
    b = lax.bitcast_convert_type(x, I32)
    return b ^ ((b >> 31) & jnp.int32(0x7FFFFFFF))


def _wide(ref):
    v = ref[...]
    return jnp.concatenate([v] * (TM // LANES), axis=1)


def _ada_kernel(c_ref, w_ref, b_ref, o_ref):
    c = c_ref[...]
    s = c * jax.nn.sigmoid(c)
    w = w_ref[...]
    s_hi = s.astype(BF16)
    s_lo = (s - s_hi.astype(F32)).astype(BF16)
    w_hi = w.astype(BF16)
    w_lo = (w - w_hi.astype(F32)).astype(BF16)
    acc = jnp.dot(s_hi, w_hi, preferred_element_type=F32)
    acc += jnp.dot(s_hi, w_lo, preferred_element_type=F32)
    acc += jnp.dot(s_lo, w_hi, preferred_element_type=F32)
    o_ref[...] = acc + b_ref[...]


def _ada_call(c_all, ada_w, ada_b):
    rows = c_all.shape[0]
    tn = 1536
    return pl.pallas_call(
        _ada_kernel,
        grid=(DEPTH, 6 * D_MODEL // tn),
        in_specs=[
            pl.BlockSpec((rows, D_MODEL), lambda l, j: (0, 0)),
            pl.BlockSpec((None, D_MODEL, tn), lambda l, j: (l, 0, j)),
            pl.BlockSpec((None, 1, tn), lambda l, j: (l, 0, j)),
        ],
        out_specs=pl.BlockSpec((None, rows, tn), lambda l, j: (l, 0, j)),
        out_shape=jax.ShapeDtypeStruct((DEPTH, rows, 6 * D_MODEL), F32),
        compiler_params=_cparams(2),
        name="adaln",
    )(c_all, ada_w, ada_b.reshape(DEPTH, 1, 6 * D_MODEL))


def _modulated(x, g_ref, sc_ref, sh_ref):
    ms = jnp.mean(x * x, axis=0, keepdims=True)
    xn = x * lax.rsqrt(ms + EPS)
    return xn * _wide(g_ref) * (1.0 + _wide(sc_ref)) + _wide(sh_ref)


def _norm_rope(blk, gain, cos, sin):
    if gain is not None:
        ms = jnp.mean(blk * blk, axis=0, keepdims=True)
        blk = blk * lax.rsqrt(ms + EPS) * gain
    x1 = blk[:HEAD_DIM // 2]
    x2 = blk[HEAD_DIM // 2:]
    return jnp.concatenate([x1 * cos - x2 * sin, x2 * cos + x1 * sin], axis=0)


def _pre_kernel(x_ref, sh_ref, sc_ref, g_ref, w_ref, cos_ref, sin_ref, gq_ref, gk_ref, *outs, dsa):
    h = _modulated(x_ref[...], g_ref, sc_ref, sh_ref)
    proj = jnp.dot(w_ref[...], h.astype(BF16), preferred_element_type=F32)
    cos = cos_ref[...]
    sin = sin_ref[...]
    gq = gq_ref[...]
    gk = gk_ref[...]
    q_ref, k_ref, v_ref = outs[:3]
    scale = HEAD_DIM ** -0.5
    for h_ in range(N_HEADS):
        r = slice(HEAD_DIM * h_, HEAD_DIM * (h_ + 1))
        q_ref[r, :] = (_norm_rope(proj[r], gq, cos, sin) * scale).astype(BF16)
    for n in range(N_KV_HEADS):
        r = slice(HEAD_DIM * n, HEAD_DIM * (n + 1))
        k_ref[r, :] = _norm_rope(proj[Q_DIM + HEAD_DIM * n:Q_DIM + HEAD_DIM * (n + 1)], gk, cos, sin)
    v_ref[...] = proj[Q_DIM + KV_DIM:QKV_DIM]
    if dsa:
        qi_ref, ki_ref, wi_ref = outs[3:]
        iscale = IDX_DIM ** -0.5
        for h_ in range(N_IDX_HEADS):
            r = slice(IDX_DIM * h_, IDX_DIM * (h_ + 1))
            qi_ref[r, :] = (_norm_rope(proj[QKV_DIM + IDX_DIM * h_:QKV_DIM + IDX_DIM * (h_ + 1)], None, cos, sin)
                            * iscale).astype(BF16)
        ki_ref[...] = _norm_rope(proj[QKV_DIM + IDXQ_DIM:QKV_DIM + IDXQ_DIM + IDX_DIM], None, cos, sin)
        wi_ref[...] = proj[QKV_DIM + IDXQ_DIM + IDX_DIM:DSA_IN_DIM] * (N_IDX_HEADS ** -0.5)


def _mod_spec(j):
    return pl.BlockSpec((None, None, D_MODEL, LANES),
                        lambda i: (jnp.minimum(i // TILES_PER_BATCH, BATCH), j, 0, 0))


def _const_spec(shape):
    return pl.BlockSpec(shape, lambda i: (0,) * len(shape), pipeline_mode=pl.Buffered(1))


def _pre_call(x_t, mod, g_b, w_t, cos, sin, gq_b, gk_b, dsa):
    n_out = w_t.shape[0]
    tok = lambda rows: pl.BlockSpec((rows, TM), lambda i: (0, i))
    out_shape = [jax.ShapeDtypeStruct((Q_DIM, TT), BF16),
                 jax.ShapeDtypeStruct((KV_DIM, TT), F32),
                 jax.ShapeDtypeStruct((KV_DIM, TT), F32)]
    out_specs = [tok(Q_DIM), tok(KV_DIM), tok(KV_DIM)]
    if dsa:
        out_shape += [jax.ShapeDtypeStruct((IDXQ_DIM, TT), BF16),
                      jax.ShapeDtypeStruct((IDX_DIM, TT), F32),
                      jax.ShapeDtypeStruct((N_IDX_HEADS, TT), F32)]
        out_specs += [tok(IDXQ_DIM), tok(IDX_DIM), tok(N_IDX_HEADS)]
    return pl.pallas_call(
        functools.partial(_pre_kernel, dsa=dsa),
        grid=(N_TILES,),
        in_specs=[tok(D_MODEL), _mod_spec(0), _mod_spec(1), _const_spec((D_MODEL, LANES)),
                  _const_spec((n_out, D_MODEL)), tok(HEAD_DIM // 2), tok(HEAD_DIM // 2),
                  _const_spec((HEAD_DIM, TM)), _const_spec((HEAD_DIM, TM))],
        out_specs=out_specs,
        out_shape=out_shape,
        compiler_params=_cparams(1),
        name="dsa_in_proj" if dsa else "swa_in_proj",
    )(x_t, mod, mod, g_b, w_t, cos, sin, gq_b, gk_b)


def _post_kernel(x_ref, o_ref, g1_ref, sh_ref, sc_ref, g2_ref, gn_ref, wo_ref, w1_ref, w2_ref, out_ref):
    a = jnp.dot(wo_ref[...], o_ref[...], preferred_element_type=F32)
    x1 = x_ref[...] + _wide(g1_ref) * a
    hb = _modulated(x1, gn_ref, sc_ref, sh_ref).astype(BF16)
    acc = jnp.zeros((D_MODEL, TM), F32)
    fc = 1024
    for c in range(D_FF // fc):
        u = jnp.dot(w1_ref[fc * c:fc * (c + 1), :], hb, preferred_element_type=F32)
        u = jnp.square(jnp.maximum(u, 0.0)).astype(BF16)
        acc = acc + jnp.dot(w2_ref[:, fc * c:fc * (c + 1)], u, preferred_element_type=F32)
    out_ref[...] = x1 + _wide(g2_ref) * acc


def _post_call(x_t, o_t, mod, gn_b, wo_t, w1_t, w2_t):
    tok = lambda rows: pl.BlockSpec((rows, TM), lambda i: (0, i))
    return pl.pallas_call(
        _post_kernel,
        grid=(N_TILES,),
        in_specs=[tok(D_MODEL), tok(Q_DIM), _mod_spec(2), _mod_spec(3), _mod_spec(4), _mod_spec(5),
                  _const_spec((D_MODEL, LANES)), _const_spec((D_MODEL, Q_DIM)),
                  _const_spec((D_FF, D_MODEL)), _const_spec((D_MODEL, D_FF))],
        out_specs=tok(D_MODEL),
        out_shape=jax.ShapeDtypeStruct((D_MODEL, TT), F32),
        compiler_params=_cparams(1),
        name="out_proj_mlp",
    )(x_t, o_t, mod, mod, mod, mod, gn_b, wo_t, w1_t, w2_t)


def _fill_qbd(qbd_ref, q_ref):
    for h_ in range(N_HEADS):
        n = h_ // GROUP
        qbd_ref[HEAD_DIM * n:HEAD_DIM * (n + 1), LANES * h_:LANES * (h_ + 1)] = \
            q_ref[HEAD_DIM * h_:HEAD_DIM * (h_ + 1), :]


def _swa_p_kernel(q_ref, kp_ref, kc_ref, vp_ref, vc_ref, sink_ref, o_ref, qbd_ref):
    n = pl.program_id(1)

    @pl.when((pl.program_id(0) == 0) & (n == 0))
    def _():
        qbd_ref[...] = jnp.zeros_like(qbd_ref)

    _fill_qbd(qbd_ref, q_ref)
    k2 = jnp.concatenate([kp_ref[...], kc_ref[...]], axis=0)
    v2 = jnp.concatenate([vp_ref[...], vc_ref[...]], axis=1)
    s = jnp.dot(k2, qbd_ref[...], preferred_element_type=F32)
    j = lax.broadcasted_iota(I32, (2 * WINDOW, LANES), 0)
    i = lax.broadcasted_iota(I32, (2 * WINDOW, LANES), 1)
    diff = i + WINDOW - j
    j_min = jnp.where(n > 0, 0, WINDOW)
    mask = (diff >= 0) & (diff < WINDOW) & (j >= j_min)
    for kv in range(N_KV_HEADS):
        ps, dens = [], []
        for g in range(GROUP):
            h_ = GROUP * kv + g
            sh = jnp.where(mask, s[:, LANES * h_:LANES * (h_ + 1)], NEG)
            sk = sink_ref[h_:h_ + 1, :]
            m = jnp.maximum(jnp.max(sh, axis=0, keepdims=True), sk)
            p = jnp.exp(sh - m)
            dens.append(jnp.sum(p, axis=0, keepdims=True) + jnp.exp(sk - m))
            ps.append(p.astype(BF16))
        pk = jnp.concatenate(ps, axis=1)
        o = jnp.dot(v2[HEAD_DIM * kv:HEAD_DIM * (kv + 1), :], pk, preferred_element_type=F32)
        for g in range(GROUP):
            h_ = GROUP * kv + g
            o_ref[HEAD_DIM * h_:HEAD_DIM * (h_ + 1), :] = (o[:, LANES * g:LANES * (g + 1)] / dens[g]).astype(BF16)


def _swa_p_call(q_t, k_nat, v_t, sink_b):
    nb = SEQ // WINDOW
    cur = lambda b, n: b * nb + n
    prev = lambda b, n: b * nb + jnp.maximum(n - 1, 0)
    return pl.pallas_call(
        _swa_p_kernel,
        grid=(BATCH, nb),
        in_specs=[pl.BlockSpec((Q_DIM, WINDOW), lambda b, n: (0, cur(b, n))),
                  pl.BlockSpec((WINDOW, KV_DIM), lambda b, n: (prev(b, n), 0)),
                  pl.BlockSpec((WINDOW, KV_DIM), lambda b, n: (cur(b, n), 0)),
                  pl.BlockSpec((KV_DIM, WINDOW), lambda b, n: (0, prev(b, n))),
                  pl.BlockSpec((KV_DIM, WINDOW), lambda b, n: (0, cur(b, n))),
                  pl.BlockSpec((N_HEADS, LANES), lambda b, n: (0, 0))],
        out_specs=pl.BlockSpec((Q_DIM, WINDOW), lambda b, n: (0, cur(b, n))),
        out_shape=jax.ShapeDtypeStruct((Q_DIM, TP), BF16),
        scratch_shapes=[pltpu.VMEM((KV_DIM, N_HEADS * LANES), BF16)],
        compiler_params=_cparams(2),
        name="swa_prompt",
    )(q_t, k_nat, k_nat, v_t, v_t, sink_b)


SWA_G = 8


def _swa_s_kernel(q_ref, ck_ref, cv_ref, kn_ref, vn_ref, sink_ref, o_ref):
    rows = DEC_SEQ * N_HEADS
    t = lax.broadcasted_iota(I32, (rows, LANES), 0) // N_HEADS
    j = lax.broadcasted_iota(I32, (rows, LANES), 1)
    mask_c = j >= t + 1
    mask_n = j <= t
    sk = sink_ref[:, :1]
    nt = (((1,), (1,)), ((), ()))
    for g in range(SWA_G):
        q = q_ref[g]
        sc = jnp.dot(q, ck_ref[g].astype(BF16), preferred_element_type=F32)
        sn = jnp.dot(q, kn_ref[g], preferred_element_type=F32)
        sc = jnp.where(mask_c, sc, NEG)
        sn = jnp.where(mask_n, sn, NEG)
        m = jnp.maximum(jnp.maximum(jnp.max(sc, axis=1, keepdims=True), jnp.max(sn, axis=1, keepdims=True)), sk)
        pc = jnp.exp(sc - m)
        pn = jnp.exp(sn - m)
        den = jnp.sum(pc, axis=1, keepdims=True) + jnp.sum(pn, axis=1, keepdims=True) + jnp.exp(sk - m)
        o = lax.dot_general(pc.astype(BF16), cv_ref[g].astype(BF16), nt, preferred_element_type=F32)
        o = o + lax.dot_general(pn.astype(BF16), vn_ref[g], nt, preferred_element_type=F32)
        o_ref[g] = o / den


def _swa_s_call(qbd, ck_t, cv_t, kn_t, vn_t, sink_rows, layer):
    rows = DEC_SEQ * N_HEADS
    seq3 = lambda r, c: pl.BlockSpec((SWA_G, r, c), lambda i: (i, 0, 0))
    cache = pl.BlockSpec((None, SWA_G, KV_DIM, WINDOW), lambda i: (layer, i, 0, 0))
    return pl.pallas_call(
        _swa_s_kernel,
        grid=(DEC_BATCH // SWA_G,),
        in_specs=[seq3(rows, KV_DIM), cache, cache, seq3(KV_DIM, LANES), seq3(KV_DIM, LANES),
                  pl.BlockSpec((rows, LANES), lambda i: (0, 0))],
        out_specs=seq3(rows, KV_DIM),
        out_shape=jax.ShapeDtypeStruct((DEC_BATCH, rows, KV_DIM), F32),
        compiler_params=_cparams(1),
        name="swa_sample",
    )(qbd, ck_t, cv_t, kn_t, vn_t, sink_rows)


def _dsa_p_kernel(q_ref, qi_ref, w_ref, ki_ref, k_ref, v_ref, o_ref,
                  keys_ref, qbd_ref, qi2_ref, m_ref, l_ref, acc_ref):
    i = pl.program_id(1)

    @pl.when((pl.program_id(0) == 0) & (i == 0))
    def _():
        qbd_ref[...] = jnp.zeros_like(qbd_ref)

    _fill_qbd(qbd_ref, q_ref)
    for h_ in range(N_IDX_HEADS):
        qi2_ref[:, LANES * h_:LANES * (h_ + 1)] = qi_ref[IDX_DIM * h_:IDX_DIM * (h_ + 1), :]

    n_chunks = (i + 2) // 2
    q_pos = i * LANES + lax.broadcasted_iota(I32, (KCH, LANES), 1)
    k_off = lax.broadcasted_iota(I32, (KCH, LANES), 0)

    def score_chunk(c, carry):
        off = pl.multiple_of(c * KCH, KCH)
        s = jnp.dot(ki_ref[pl.ds(off, KCH), :], qi2_ref[...], preferred_element_type=F32)
        sc = jnp.zeros((KCH, LANES), F32)
        for h_ in range(N_IDX_HEADS):
            sc = sc + jnp.maximum(s[:, LANES * h_:LANES * (h_ + 1)], 0.0) * w_ref[h_:h_ + 1, :]
        key = jnp.where(off + k_off <= q_pos, _sortable(sc), INT_MIN)
        keys_ref[pl.ds(off, KCH), :] = key
        return carry

    lax.fori_loop(0, n_chunks, score_chunk, 0)

    def count_ge(cand):
        def body(c, acc):
            off = pl.multiple_of(c * KCH, KCH)
            ge = (keys_ref[pl.ds(off, KCH), :] >= cand).astype(I32)
            return acc + jnp.sum(ge.reshape(KCH // 8, 8, LANES), axis=0)
        acc = lax.fori_loop(0, n_chunks, body, jnp.zeros((8, LANES), I32))
        return jnp.sum(acc, axis=0, keepdims=True)

    zero = jnp.zeros((1, LANES), I32)
    ans = jnp.where(count_ge(zero) >= TOPK, zero, INT_MIN)

    def bit_step(t, ans):
        cand = ans | lax.shift_left(jnp.int32(1), 30 - t)
        return jnp.where(count_ge(cand) >= TOPK, cand, ans)

    thr = jnp.maximum(lax.fori_loop(0, 31, bit_step, ans), INT_MIN + 1)

    m_ref[...] = jnp.full_like(m_ref, NEG)
    l_ref[...] = jnp.zeros_like(l_ref)
    acc_ref[...] = jnp.zeros_like(acc_ref)

    def attend_chunk(c, carry):
        off = pl.multiple_of(c * KCH, KCH)
        s = jnp.dot(k_ref[pl.ds(off, KCH), :], qbd_ref[...], preferred_element_type=F32)
        sel = keys_ref[pl.ds(off, KCH), :] >= thr
        vc = v_ref[:, pl.ds(off, KCH)]
        for kv in range(N_KV_HEADS):
            ps, alphas = [], []
            for g in range(GROUP):
                h_ = GROUP * kv + g
                sh = jnp.where(sel, s[:, LANES * h_:LANES * (h_ + 1)], NEG)
                m_old = m_ref[h_:h_ + 1, :]
                m_new = jnp.maximum(m_old, jnp.max(sh, axis=0, keepdims=True))
                alpha = jnp.exp(m_old - m_new)
                p = jnp.exp(sh - m_new)
                l_ref[h_:h_ + 1, :] = alpha * l_ref[h_:h_ + 1, :] + jnp.sum(p, axis=0, keepdims=True)
                m_ref[h_:h_ + 1, :] = m_new
                ps.append(p.astype(BF16))
                alphas.append(alpha)
            pk = jnp.concatenate(ps, axis=1)
            a4 = jnp.concatenate(alphas, axis=1)
            pv = jnp.dot(vc[HEAD_DIM * kv:HEAD_DIM * (kv + 1), :], pk, preferred_element_type=F32)
            acc_ref[kv] = acc_ref[kv] * a4 + pv
        return carry

    lax.fori_loop(0, n_chunks, attend_chunk, 0)

    for h_ in range(N_HEADS):
        kv, g = h_ // GROUP, h_ % GROUP
        o_ref[HEAD_DIM * h_:HEAD_DIM * (h_ + 1), :] = \
            (acc_ref[kv][:, LANES * g:LANES * (g + 1)] / l_ref[h_:h_ + 1, :]).astype(BF16)


def _dsa_p_call(q_t, qi_t, w_t, ki_nat, k_nat, v_t):
    nb = SEQ // LANES
    col = lambda rows: pl.BlockSpec((rows, LANES), lambda b, i: (0, b * nb + i))
    return pl.pallas_call(
        _dsa_p_kernel,
        grid=(BATCH, nb),
        in_specs=[col(Q_DIM), col(IDXQ_DIM), col(N_IDX_HEADS),
                  pl.BlockSpec((None, SEQ, IDX_DIM), lambda b, i: (b, 0, 0)),
                  pl.BlockSpec((None, SEQ, KV_DIM), lambda b, i: (b, 0, 0)),
                  pl.BlockSpec((KV_DIM, SEQ), lambda b, i: (0, b))],
        out_specs=col(Q_DIM),
        out_shape=jax.ShapeDtypeStruct((Q_DIM, TP), BF16),
        scratch_shapes=[pltpu.VMEM((SEQ, LANES), I32),
                        pltpu.VMEM((KV_DIM, N_HEADS * LANES), BF16),
                        pltpu.VMEM((IDX_DIM, N_IDX_HEADS * LANES), BF16),
                        pltpu.VMEM((N_HEADS, LANES), F32),
                        pltpu.VMEM((N_HEADS, LANES), F32),
                        pltpu.VMEM((N_KV_HEADS, HEAD_DIM, GROUP * LANES), F32)],
        compiler_params=_cparams(2),
        name="dsa_prompt",
    )(q_t, qi_t, w_t, ki_nat, k_nat, v_t)


IDX_G = 2
IDX_ROWS = N_IDX_HEADS * 8


def _page_copy(pool_ref, layer, page, dst, sem):
    return pltpu.make_async_copy(pool_ref.at[layer, page], dst, sem)


def _dsa_si_kernel(pt_ref, qi_ref, wn_ref, kin_ref, pool_ref, o_ref, kbuf, sem, *, layer):
    step = pl.program_id(0)
    n_steps = pl.num_programs(0)

    def fetch(st, slot):
        for g in range(IDX_G):
            def body(p, carry):
                _page_copy(pool_ref, layer, pt_ref[st * IDX_G + g, p], kbuf.at[slot, g, p], sem.at[slot]).start()
                return carry
            lax.fori_loop(0, N_PAGES, body, 0)

    def wait(slot):
        for g in range(IDX_G):
            def body(p, carry):
                _page_copy(pool_ref, layer, 0, kbuf.at[slot, g, p], sem.at[slot]).wait()
                return carry
            lax.fori_loop(0, N_PAGES, body, 0)

    @pl.when(step == 0)
    def _():
        fetch(0, 0)

    slot = step % 2

    @pl.when(step + 1 < n_steps)
    def _():
        fetch(step + 1, 1 - slot)

    wait(slot)

    t_row = lax.broadcasted_iota(I32, (8, LANES), 0)
    t_col = lax.broadcasted_iota(I32, (8, LANES), 1)
    new_ok = (t_col <= t_row) & (t_row < DEC_SEQ)
    for g in range(IDX_G):
        q = qi_ref[g]
        wn = wn_ref[g]

        def scores(kt):
            s = jnp.dot(q, kt, preferred_element_type=F32)
            n = kt.shape[1]
            wfull = wn if n == LANES else jnp.concatenate([wn] * (n // LANES), axis=1)
            r = jnp.maximum(s, 0.0) * wfull
            return jnp.sum(r.reshape(N_IDX_HEADS, 8, n), axis=0)

        def body(c, carry):
            kt = jnp.concatenate([kbuf[slot, g, 2 * c], kbuf[slot, g, 2 * c + 1]], axis=1).astype(BF16)
            key = _sortable(scores(kt))
            off = pl.multiple_of(c * 2 * LANES, 2 * LANES)
            o_ref[g, :, pl.ds(off, 2 * LANES)] = key[:DEC_SEQ]
            return carry

        lax.fori_loop(0, N_PAGES // 2, body, 0)
        key_n = jnp.where(new_ok, _sortable(scores(kin_ref[g])), INT_MIN)
        o_ref[g, :, PAST_LEN:SKEYS] = key_n[:DEC_SEQ]


def _dsa_si_call(page_table, qi_rows, wn_rows, kin_t, pool_kidx_t, layer):
    seq3 = lambda r, c: pl.BlockSpec((IDX_G, r, c), lambda i, pt: (i, 0, 0))
    return pl.pallas_call(
        functools.partial(_dsa_si_kernel, layer=layer),
        grid_spec=pltpu.PrefetchScalarGridSpec(
            num_scalar_prefetch=1,
            grid=(DEC_BATCH // IDX_G,),
            in_specs=[seq3(IDX_ROWS, IDX_DIM), seq3(IDX_ROWS, LANES), seq3(IDX_DIM, LANES),
                      pl.BlockSpec(memory_space=pl.ANY)],
            out_specs=seq3(DEC_SEQ, SKEYS),
            scratch_shapes=[pltpu.VMEM((2, IDX_G, N_PAGES, IDX_DIM, PAGE_SIZE), F32),
                            pltpu.SemaphoreType.DMA((2,))],
        ),
        out_shape=jax.ShapeDtypeStruct((DEC_BATCH, DEC_SEQ, SKEYS), I32),
        compiler_params=_cparams(1),
        name="dsa_sample_index",
    )(page_table, qi_rows, wn_rows, kin_t, pool_kidx_t)


THR_ROWS = 64


def _thr_kernel(k_ref, o_ref):
    n_tiles = SKEYS // LANES

    def count_ge(cand):
        acc = jnp.zeros((THR_ROWS, LANES), I32)
        for j in range(n_tiles):
            acc = acc + (k_ref[:, LANES * j:LANES * (j + 1)] >= cand).astype(I32)
        return jnp.broadcast_to(jnp.sum(acc, axis=1, keepdims=True), (THR_ROWS, LANES))

    zero = jnp.zeros((THR_ROWS, LANES), I32)
    ans = jnp.where(count_ge(zero) >= TOPK, zero, INT_MIN)

    def bit_step(t, ans):
        cand = ans | lax.shift_left(jnp.int32(1), 30 - t)
        return jnp.where(count_ge(cand) >= TOPK, cand, ans)

    o_ref[...] = jnp.maximum(lax.fori_loop(0, 31, bit_step, ans), INT_MIN + 1)


def _thr_call(keys2d):
    return pl.pallas_call(
        _thr_kernel,
        grid=(TS // THR_ROWS,),
        in_specs=[pl.BlockSpec((THR_ROWS, SKEYS), lambda i: (i, 0))],
        out_specs=pl.BlockSpec((THR_ROWS, LANES), lambda i: (i, 0)),
        out_shape=jax.ShapeDtypeStruct((TS, LANES), I32),
        compiler_params=_cparams(1),
        name="dsa_sample_threshold",
    )(keys2d)


CH_PAGES = 8
N_CH = N_PAGES // CH_PAGES


def _dsa_sa_kernel(pt_ref, q_ref, keys_ref, thr_ref, kn_ref, vn_ref, kpool_ref, vpool_ref, o_ref,
                   kbuf, vbuf, sem, *, layer):
    b = pl.program_id(0)
    n_seq = pl.num_programs(0)
    rows = DEC_SEQ * N_HEADS
    nt = (((1,), (1,)), ((), ()))

    def copies(seq, c, slot):
        out = []
        for p in range(CH_PAGES):
            page = pt_ref[seq, c * CH_PAGES + p]
            out.append(_page_copy(kpool_ref, layer, page, kbuf.at[slot, p], sem.at[0, slot]))
            out.append(_page_copy(vpool_ref, layer, page, vbuf.at[slot, p], sem.at[1, slot]))
        return out

    def fetch(seq, c, slot):
        for cp in copies(seq, c, slot):
            cp.start()

    def wait(slot):
        for cp in copies(0, 0, slot):
            cp.wait()

    @pl.when(b == 0)
    def _():
        fetch(0, 0, 0)

    q = q_ref[...]
    thr = thr_ref[...]

    def expand(sel):
        n = sel.shape[1]
        return jnp.concatenate([jnp.broadcast_to(sel[t:t + 1], (N_HEADS, n)) for t in range(DEC_SEQ)], axis=0)

    def block(kt, vt, key_blk, m, l, acc):
        n = kt.shape[1]
        thr_n = thr if n == LANES else jnp.concatenate([thr] * (n // LANES), axis=1)
        s = jnp.dot(q, kt, preferred_element_type=F32)
        s = jnp.where(expand((key_blk >= thr_n).astype(I32)) > 0, s, NEG)
        m_new = jnp.maximum(m, jnp.max(s, axis=1, keepdims=True))
        alpha = jnp.exp(m - m_new)
        p = jnp.exp(s - m_new)
        l = alpha * l + jnp.sum(p, axis=1, keepdims=True)
        acc = alpha * acc + lax.dot_general(p.astype(BF16), vt, nt, preferred_element_type=F32)
        return m_new, l, acc

    def chunk(c, carry):
        m, l, acc = carry
        g = b * N_CH + c
        slot = g % 2
        nxt = g + 1

        @pl.when(nxt < n_seq * N_CH)
        def _():
            fetch(nxt // N_CH, nxt % N_CH, 1 - slot)

        wait(slot)
        for h_ in range(CH_PAGES // 2):
            kt = jnp.concatenate([kbuf[slot, 2 * h_], kbuf[slot, 2 * h_ + 1]], axis=1).astype(BF16)
            vt = jnp.concatenate([vbuf[slot, 2 * h_], vbuf[slot, 2 * h_ + 1]], axis=1).astype(BF16)
            off = pl.multiple_of(c * (CH_PAGES * PAGE_SIZE) + h_ * 2 * LANES, 2 * LANES)
            m, l, acc = block(kt, vt, keys_ref[:, pl.ds(off, 2 * LANES)], m, l, acc)
        return m, l, acc

    init = (jnp.full((rows, 1), NEG, F32), jnp.zeros((rows, 1), F32), jnp.zeros((rows, KV_DIM), F32))
    m, l, acc = lax.fori_loop(0, N_CH, chunk, init)
    m, l, acc = block(kn_ref[...], vn_ref[...], keys_ref[:, PAST_LEN:SKEYS], m, l, acc)
    o_ref[...] = acc / l


def _dsa_sa_call(page_table, qbd, keys, thr, kn_t, vn_t, kpool_t, vpool_t, layer):
    rows = DEC_SEQ * N_HEADS
    seq3 = lambda r, c: pl.BlockSpec((None, r, c), lambda i, pt: (i, 0, 0))
    return pl.pallas_call(
        functools.partial(_dsa_sa_kernel, layer=layer),
        grid_spec=pltpu.PrefetchScalarGridSpec(
            num_scalar_prefetch=1,
            grid=(DEC_BATCH,),
            in_specs=[seq3(rows, KV_DIM), seq3(DEC_SEQ, SKEYS), seq3(DEC_SEQ, LANES),
                      seq3(KV_DIM, LANES), seq3(KV_DIM, LANES),
                      pl.BlockSpec(memory_space=pl.ANY), pl.BlockSpec(memory_space=pl.ANY)],
            out_specs=seq3(rows, KV_DIM),
            scratch_shapes=[pltpu.VMEM((2, CH_PAGES, KV_DIM, PAGE_SIZE), F32),
                            pltpu.VMEM((2, CH_PAGES, KV_DIM, PAGE_SIZE), F32),
                            pltpu.SemaphoreType.DMA((2, 2))],
        ),
        out_shape=jax.ShapeDtypeStruct((DEC_BATCH, rows, KV_DIM), F32),
        compiler_params=_cparams(1),
        name="dsa_sample_attention",
    )(page_table, qbd, keys, thr, kn_t, vn_t, kpool_t, vpool_t)


def _lane_bcast(v, n=LANES):
    return jnp.broadcast_to(v[..., None], v.shape + (n,))


def _sample_rows(a_t, rows):
    return a_t[:, TP:].reshape(rows, DEC_SEQ, DEC_BATCH)


def _new_key_pages(a_t):
    a = jnp.transpose(_sample_rows(a_t, a_t.shape[0]), (2, 0, 1))
    return jnp.pad(a, ((0, 0), (0, 0), (0, LANES - DEC_SEQ))).astype(BF16)


def _sample_qbd(q_t):
    q = jnp.transpose(q_t[:, TP:].reshape(N_HEADS, HEAD_DIM, DEC_SEQ, DEC_BATCH), (3, 2, 0, 1))
    onehot = (jnp.arange(N_HEADS)[:, None] // GROUP == jnp.arange(N_KV_HEADS)[None, :]).astype(q.dtype)
    qbd = q[:, :, :, None, :] * onehot[None, None, :, :, None]
    return qbd.reshape(DEC_BATCH, DEC_SEQ * N_HEADS, KV_DIM)


def _sample_out_t(o_rows):
    o = o_rows.reshape(DEC_BATCH, DEC_SEQ, N_KV_HEADS, GROUP, N_KV_HEADS, HEAD_DIM)
    eye = jnp.eye(N_KV_HEADS, dtype=o.dtype)
    o = jnp.sum(o * eye[None, None, :, None, :, None], axis=4)
    return jnp.transpose(o, (2, 3, 4, 1, 0)).reshape(Q_DIM, TS).astype(BF16)


def _heads_last(a):
    l, b, _, s = a.shape
    return jnp.transpose(a.reshape(l, b, N_KV_HEADS, HEAD_DIM, s), (0, 1, 4, 2, 3))


def kernel(x_prompt, x_sample, cache_swa_k, cache_swa_v, cache_dsa_k, cache_dsa_v, cache_dsa_kidx, page_table,
           c_prompt, c_sample, ada_w, ada_b, norm1_g, norm2_g, swa_w_in, swa_q_norm, swa_k_norm, swa_sinks,
           swa_w_out, dsa_w_in, dsa_q_norm, dsa_k_norm, dsa_w_out, mlp_w1, mlp_w2):
    x_t = jnp.concatenate([jnp.transpose(x_prompt, (2, 0, 1)).reshape(D_MODEL, TP),
                           jnp.transpose(x_sample, (2, 1, 0)).reshape(D_MODEL, TS)], axis=1)

    pos = jnp.concatenate([jnp.tile(jnp.arange(SEQ), BATCH), PAST_LEN + jnp.repeat(jnp.arange(DEC_SEQ), DEC_BATCH)])
    d2 = HEAD_DIM // 2
    inv = ROPE_THETA ** (-jnp.arange(d2, dtype=F32) / d2)
    ang = inv[:, None] * pos.astype(F32)[None, :]
    cos, sin = jnp.cos(ang), jnp.sin(ang)

    n_c = DEC_BATCH + 8
    c_all = jnp.zeros((n_c, D_MODEL), F32).at[:DEC_BATCH].set(c_sample).at[DEC_BATCH:DEC_BATCH + BATCH].set(c_prompt)
    mod_rows = _ada_call(c_all, ada_w, ada_b).reshape(DEPTH, n_c, 6, D_MODEL)
    mod_s = jnp.transpose(mod_rows[:, :DEC_BATCH], (0, 2, 3, 1))[:, None]
    mod_p = _lane_bcast(mod_rows[:, DEC_BATCH:DEC_BATCH + BATCH])
    mod = jnp.concatenate([mod_p, mod_s], axis=1)

    swa_ck = jnp.transpose(cache_swa_k, (0, 1, 3, 4, 2)).reshape(-1, DEC_BATCH, KV_DIM, WINDOW)
    swa_cv = jnp.transpose(cache_swa_v, (0, 1, 3, 4, 2)).reshape(-1, DEC_BATCH, KV_DIM, WINDOW)
    n_pool = cache_dsa_k.shape[1]
    dsa_ck = jnp.transpose(cache_dsa_k, (0, 1, 3, 4, 2)).reshape(-1, n_pool, KV_DIM, PAGE_SIZE)
    dsa_cv = jnp.transpose(cache_dsa_v, (0, 1, 3, 4, 2)).reshape(-1, n_pool, KV_DIM, PAGE_SIZE)
    dsa_ci = jnp.transpose(cache_dsa_kidx, (0, 1, 3, 2))

    swa_kp, swa_vp, swa_ks, swa_vs = [], [], [], []
    dsa_kp, dsa_vp, dsa_ip, dsa_ks, dsa_vs, dsa_is = [], [], [], [], [], []
    for i in range(DEPTH):
        j = i // 2
        g1_b = _lane_bcast(norm1_g[i])
        g2_b = _lane_bcast(norm2_g[i])
        if i % 2 == 0:
            w_t = jnp.transpose(swa_w_in[j]).astype(BF16)
            gq_b = _lane_bcast(swa_q_norm[j], TM)
            gk_b = _lane_bcast(swa_k_norm[j], TM)
            q_t, k_t, v_t = _pre_call(x_t, mod[i], g1_b, w_t, cos, sin, gq_b, gk_b, dsa=False)
            k_nat = jnp.transpose(k_t[:, :TP]).astype(BF16)
            o_p = _swa_p_call(q_t, k_nat, v_t.astype(BF16), _lane_bcast(swa_sinks[j]))
            sink_rows = _lane_bcast(jnp.tile(swa_sinks[j], DEC_SEQ))
            kn_t, vn_t = _new_key_pages(k_t), _new_key_pages(v_t)
            o_s = _swa_s_call(_sample_qbd(q_t), swa_ck, swa_cv, kn_t, vn_t, sink_rows, j)
            k_p = k_t[:, :TP].reshape(KV_DIM, BATCH, SEQ)[:, :, SEQ - WINDOW:]
            v_p = v_t[:, :TP].reshape(KV_DIM, BATCH, SEQ)[:, :, SEQ - WINDOW:]
            swa_kp.append(jnp.transpose(k_p, (1, 0, 2)))
            swa_vp.append(jnp.transpose(v_p, (1, 0, 2)))
            k_new = jnp.transpose(_sample_rows(k_t, KV_DIM), (2, 0, 1))
            v_new = jnp.transpose(_sample_rows(v_t, KV_DIM), (2, 0, 1))
            swa_ks.append(jnp.concatenate([swa_ck[j][:, :, DEC_SEQ:], k_new], axis=2))
            swa_vs.append(jnp.concatenate([swa_cv[j][:, :, DEC_SEQ:], v_new], axis=2))
            wo_t = jnp.transpose(swa_w_out[j]).astype(BF16)
        else:
            w_t = jnp.pad(jnp.transpose(dsa_w_in[j]), ((0, DSA_ROWS - DSA_IN_DIM), (0, 0))).astype(BF16)
            gq_b = _lane_bcast(dsa_q_norm[j], TM)
            gk_b = _lane_bcast(dsa_k_norm[j], TM)
            q_t, k_t, v_t, qi_t, ki_t, wi_t = _pre_call(x_t, mod[i], g1_b, w_t, cos, sin, gq_b, gk_b, dsa=True)
            k_nat = jnp.transpose(k_t[:, :TP]).astype(BF16).reshape(BATCH, SEQ, KV_DIM)
            ki_nat = jnp.transpose(ki_t[:, :TP]).astype(BF16).reshape(BATCH, SEQ, IDX_DIM)
            o_p = _dsa_p_call(q_t, qi_t, wi_t, ki_nat, k_nat, v_t.astype(BF16))
            qi_rows = jnp.transpose(qi_t[:, TP:].reshape(N_IDX_HEADS, IDX_DIM, DEC_SEQ, DEC_BATCH), (3, 0, 2, 1))
            qi_rows = jnp.pad(qi_rows, ((0, 0), (0, 0), (0, 8 - DEC_SEQ), (0, 0))).reshape(DEC_BATCH, IDX_ROWS, IDX_DIM)
            wn_rows = jnp.transpose(_sample_rows(wi_t, N_IDX_HEADS), (2, 0, 1))
            wn_rows = _lane_bcast(jnp.pad(wn_rows, ((0, 0), (0, 0), (0, 8 - DEC_SEQ))).reshape(DEC_BATCH, IDX_ROWS))
            kin_t = _new_key_pages(ki_t)
            keys = _dsa_si_call(page_table, qi_rows, wn_rows, kin_t, dsa_ci, j)
            thr = _thr_call(keys.reshape(TS, SKEYS)).reshape(DEC_BATCH, DEC_SEQ, LANES)
            kn_t, vn_t = _new_key_pages(k_t), _new_key_pages(v_t)
            o_s = _dsa_sa_call(page_table, _sample_qbd(q_t), keys, thr, kn_t, vn_t, dsa_ck, dsa_cv, j)
            dsa_kp.append(jnp.transpose(k_t[:, :TP].reshape(KV_DIM, BATCH, SEQ), (1, 0, 2)))
            dsa_vp.append(jnp.transpose(v_t[:, :TP].reshape(KV_DIM, BATCH, SEQ), (1, 0, 2)))
            dsa_ip.append(jnp.transpose(ki_t[:, :TP].reshape(IDX_DIM, BATCH, SEQ), (1, 2, 0)))
            dsa_ks.append(jnp.transpose(_sample_rows(k_t, KV_DIM), (2, 0, 1)))
            dsa_vs.append(jnp.transpose(_sample_rows(v_t, KV_DIM), (2, 0, 1)))
            dsa_is.append(jnp.transpose(_sample_rows(ki_t, IDX_DIM), (2, 1, 0)))
            wo_t = jnp.transpose(dsa_w_out[j]).astype(BF16)
        o_t = jnp.concatenate([o_p, _sample_out_t(o_s)], axis=1)
        w1_t = jnp.transpose(mlp_w1[i]).astype(BF16)
        w2_t = jnp.transpose(mlp_w2[i]).astype(BF16)
        x_t = _post_call(x_t, o_t, mod[i], g2_b, wo_t, w1_t, w2_t)

    y_p = jnp.transpose(x_t[:, :TP].reshape(D_MODEL, BATCH, SEQ), (1, 2, 0))
    y_s = jnp.transpose(x_t[:, TP:].reshape(D_MODEL, DEC_SEQ, DEC_BATCH), (2, 1, 0))
    return (y_p, y_s,
            _heads_last(jnp.stack(swa_kp)), _heads_last(jnp.stack(swa_vp)),
            _heads_last(jnp.stack(swa_ks)), _heads_last(jnp.stack(swa_vs)),
            _heads_last(jnp.stack(dsa_kp)), _heads_last(jnp.stack(dsa_vp)), jnp.stack(dsa_ip),
            _heads_last(jnp.stack(dsa_ks)), _heads_last(jnp.stack(dsa_vs)), jnp.stack(dsa_is))
```
